```python
import math
import jax, jax.numpy as jnp
from jax import lax
import numpy as np

D_MODEL = 1024
BATCH = 1
SEQ = 16384
DEPTH = 4
DEC_BATCH = 32
DEC_SEQ = 2048
PAST_LEN = 128

N_MIXERS = 3
GRID_W = 64
Q_BLOCK = 128
EPS = 1e-6
DIFF_HEADS = 8
DIFF_DH = 64
DIFF_EPS = 1e-5
NUM_BUCKETS = 32
MAX_DISTANCE = 128
NA_HEADS = 16
NA_DH = 64
NA_KH_MAX = 8
NA_KW = 16
MLA_HEADS = 16
MLA_Q_RANK = 384
MLA_KV_RANK = 256
MLA_NOPE = 64
MLA_ROPE = 32
MLA_V = 64
ROPE_THETA = 10000.0
N_EXPERTS = 16
EC_CAPACITY_FACTOR = 2
EXPERT_FF = 1024

kernel_name = "hybrid_diff_natten_mla_ec_encoder"


def _rmsnorm(x, g, eps=EPS):
    xf = x.astype(jnp.float32)
    y = xf * lax.rsqrt(jnp.mean(xf * xf, axis=-1, keepdims=True) + eps)
    return (y * g.astype(jnp.float32)).astype(x.dtype)


def _t5_bucket(rel):
    half = NUM_BUCKETS // 2
    max_exact = half // 2
    ret = jnp.where(rel > 0, half, 0)
    n = jnp.abs(rel)
    nf = jnp.maximum(n, 1).astype(jnp.float32)
    large = max_exact + (jnp.log(nf / max_exact) / math.log(MAX_DISTANCE / max_exact)
                         * (half - max_exact)).astype(jnp.int32)
    large = jnp.minimum(large, half - 1)
    return ret + jnp.where(n < max_exact, n, large)


def _rope_angles(S):
    pos = jnp.arange(S, dtype=jnp.float32)
    inv = ROPE_THETA ** (-jnp.arange(0, MLA_ROPE, 2, dtype=jnp.float32) / MLA_ROPE)
    ang = pos[:, None] * inv[None, :]
    return jnp.cos(ang), jnp.sin(ang)


def _apply_rope(x, cos, sin):
    half = x.shape[-1] // 2
    xf = x.astype(jnp.float32)
    x1, x2 = xf[..., :half], xf[..., half:]
    return jnp.concatenate([x1 * cos - x2 * sin, x1 * sin + x2 * cos], axis=-1).astype(x.dtype)


def _diff_attn(h, w_qkv, w_o, lam, subln, t5_bias, lambda_init):
    B, S, _ = h.shape
    qkv = jnp.einsum('bsd,de->bse', h, w_qkv)
    q, k, v = jnp.split(qkv, 3, axis=-1)
    q = q.reshape(B, S, DIFF_HEADS, 2, DIFF_DH)
    k = k.reshape(B, S, DIFF_HEADS, 2, DIFF_DH)
    v = v.reshape(B, S, DIFF_HEADS, 2 * DIFF_DH)
    lamf = lam.astype(jnp.float32)
    lam_full = (jnp.exp(jnp.sum(lamf[0] * lamf[1])) - jnp.exp(jnp.sum(lamf[2] * lamf[3]))
                + lambda_init)
    n_blk = S // Q_BLOCK
    kpos = jnp.arange(S, dtype=jnp.int32)
    starts = jnp.arange(n_blk, dtype=jnp.int32) * Q_BLOCK
    scale = DIFF_DH ** -0.5

    def one_seq(args):
        qs, ks, vs = args
        qb = qs.reshape(n_blk, Q_BLOCK, DIFF_HEADS, 2, DIFF_DH)

        def one_block(blk):
            qi, start = blk
            s = jnp.einsum('qhjd,khjd->hjqk', qi, ks).astype(jnp.float32) * scale
            rel = kpos[None, :] - (start + jnp.arange(Q_BLOCK, dtype=jnp.int32))[:, None]
            bias = t5_bias[_t5_bucket(rel)]
            s = s + jnp.transpose(bias, (2, 3, 0, 1)).astype(jnp.float32)
            p = jax.nn.softmax(s, axis=-1)
            a = p[:, 0] - lam_full * p[:, 1]
            return jnp.einsum('hqk,khe->qhe', a.astype(vs.dtype), vs)

        return lax.map(one_block, (qb, starts)).reshape(S, DIFF_HEADS, 2 * DIFF_DH)

    o = lax.map(one_seq, (q, k, v))
    o = _rmsnorm(o, subln, DIFF_EPS) * (1.0 - lambda_init)
    return jnp.einsum('bse,ed->bsd', o.reshape(B, S, DIFF_HEADS * 2 * DIFF_DH), w_o)


def _na_attn(h, w_qkv, w_o, rpb):
    B, S, _ = h.shape
    rows = S // GRID_W
    kh = min(NA_KH_MAX, rows)
    qkv = jnp.einsum('bsd,de->bse', h, w_qkv).reshape(B, rows, GRID_W, 3, NA_HEADS, NA_DH)
    q, k, v = qkv[..., 0, :, :], qkv[..., 1, :, :], qkv[..., 2, :, :]
    qc = jnp.arange(GRID_W, dtype=jnp.int32)
    cs = jnp.clip(qc - NA_KW // 2, 0, GRID_W - NA_KW)
    col_idx = cs[:, None] + jnp.arange(NA_KW, dtype=jnp.int32)[None, :]
    dc = col_idx - qc[:, None] + (NA_KW - 1)
    qr = jnp.arange(rows, dtype=jnp.int32)
    rs = jnp.clip(qr - kh // 2, 0, rows - kh)
    scale = NA_DH ** -0.5

    def one_seq(args):
        qs, ks, vs = args

        def one_row(rargs):
            qrow, r, r0 = rargs
            kband = lax.dynamic_slice_in_dim(ks, r0, kh, axis=0)
            vband = lax.dynamic_slice_in_dim(vs, r0, kh, axis=0)
            kn = kband[:, col_idx]
            vn = vband[:, col_idx]
            s = jnp.einsum('qhd,aqchd->hqac', qrow, kn).astype(jnp.float32) * scale
            dr = r0 + jnp.arange(kh, dtype=jnp.int32) - r + (NA_KH_MAX - 1)
            b = rpb[:, dr[:, None, None], dc[None, :, :]]
            s = s + jnp.transpose(b, (0, 2, 1, 3)).astype(jnp.float32)
            p = jax.nn.softmax(s.reshape(NA_HEADS, GRID_W, kh * NA_KW), axis=-1)
            p = p.reshape(NA_HEADS, GRID_W, kh, NA_KW).astype(vn.dtype)
            return jnp.einsum('hqac,aqchd->qhd', p, vn)

        return lax.map(one_row, (qs, qr, rs))

    o = lax.map(one_seq, (q, k, v))
    return jnp.einsum('bse,ed->bsd', o.reshape(B, S, NA_HEADS * NA_DH), w_o)


def _mla(h, w_down, q_norm, kv_norm, w_uq, w_ukv, w_o):
    B, S, _ = h.shape
    down = jnp.einsum('bsd,de->bse', h, w_down)
    c_q = _rmsnorm(down[..., :MLA_Q_RANK], q_norm)
    c_kv = _rmsnorm(down[..., MLA_Q_RANK:MLA_Q_RANK + MLA_KV_RANK], kv_norm)
    k_rope = down[..., MLA_Q_RANK + MLA_KV_RANK:]
    q = jnp.einsum('bsr,re->bse', c_q, w_uq).reshape(B, S, MLA_HEADS, MLA_NOPE + MLA_ROPE)
    kv = jnp.einsum('bsr,re->bse', c_kv, w_ukv).reshape(B, S, MLA_HEADS, MLA_NOPE + MLA_V)
    q_nope, q_rope = q[..., :MLA_NOPE], q[..., MLA_NOPE:]
    k_nope, v = kv[..., :MLA_NOPE], kv[..., MLA_NOPE:]
    cos, sin = _rope_angles(S)
    q_rope = _apply_rope(q_rope, cos[:, None, :], sin[:, None, :])
    k_rope = _apply_rope(k_rope, cos, sin)
    n_blk = S // Q_BLOCK
    scale = (MLA_NOPE + MLA_ROPE) ** -0.5

    def one_seq(args):
        qn, qr, kn, kr, vs = args
        qnb = qn.reshape(n_blk, Q_BLOCK, MLA_HEADS, MLA_NOPE)
        qrb = qr.reshape(n_blk, Q_BLOCK, MLA_HEADS, MLA_ROPE)

        def one_block(blk):
            qni, qri = blk
            s = (jnp.einsum('qhd,khd->hqk', qni, kn)
                 + jnp.einsum('qhr,kr->hqk', qri, kr)).astype(jnp.float32) * scale
            p = jax.nn.softmax(s, axis=-1)
            return jnp.einsum('hqk,khe->qhe', p.astype(vs.dtype), vs)

        return lax.map(one_block, (qnb, qrb)).reshape(S, MLA_HEADS, MLA_V)

    o = lax.map(one_seq, (q_nope, q_rope, k_nope, k_rope, v))
    return jnp.einsum('bse,ed->bsd', o.reshape(B, S, MLA_HEADS * MLA_V), w_o)


def _expert_choice(h, router, w_gate, w_up, w_down):
    B, S, D = h.shape
    xt = h.reshape(B * S, D)
    n_tok = B * S
    cap = EC_CAPACITY_FACTOR * n_tok // N_EXPERTS
    aff = jax.nn.softmax(jnp.einsum('nd,de->ne', xt, router).astype(jnp.float32), axis=-1)
    gates, idx = lax.top_k(aff.T, cap)
    xe = xt[idx]
    g = jnp.einsum('ecd,edf->ecf', xe, w_gate)
    u = jnp.einsum('ecd,edf->ecf', xe, w_up)
    ye = jnp.einsum('ecf,efd->ecd', jax.nn.silu(g) * u, w_down)
    ye = ye * gates[..., None].astype(ye.dtype)
    y = jnp.zeros_like(xt).at[idx.reshape(-1)].add(ye.reshape(-1, D))
    return y.reshape(B, S, D)


def _trunk(x, norm_mix, norm_ffn, norm_final, t5_bias,
           diff_w_qkv, diff_w_o, diff_lambda, diff_subln,
           na_w_qkv, na_w_o, na_rpb,
           mla_w_down, mla_q_norm, mla_kv_norm, mla_w_uq, mla_w_ukv, mla_w_o,
           moe_router, moe_w_gate, moe_w_up, moe_w_down):
    for i in range(DEPTH):
        m, j = i % N_MIXERS, i // N_MIXERS
        h = _rmsnorm(x, norm_mix[i])
        if m == 0:
            lambda_init = 0.8 - 0.6 * math.exp(-0.3 * i)
            h = _diff_attn(h, diff_w_qkv[j], diff_w_o[j], diff_lambda[j], diff_subln[j],
                           t5_bias, lambda_init)
        elif m == 1:
            h = _na_attn(h, na_w_qkv[j], na_w_o[j], na_rpb[j])
        else:
            h = _mla(h, mla_w_down[j], mla_q_norm[j], mla_kv_norm[j], mla_w_uq[j],
                     mla_w_ukv[j], mla_w_o[j])
        x = x + h
        x = x + _expert_choice(_rmsnorm(x, norm_ffn[i]), moe_router[i], moe_w_gate[i],
                               moe_w_up[i], moe_w_down[i])
    return _rmsnorm(x, norm_final)


def setup_inputs(seed: int = 0) -> dict:
    key = jax.random.key(seed)
    ks = iter(jax.random.split(key, 32))
    d = D_MODEL
    n_a, n_b, n_c = (len(range(m, DEPTH, N_MIXERS)) for m in range(N_MIXERS))

    def nrm(shape, scale):
        return jax.random.normal(next(ks), shape, jnp.float32) * scale

    def gain(shape):
        return 1.0 + 0.05 * jax.random.normal(next(ks), shape, jnp.float32)

    inp = {}
    inp['x_prompt'] = nrm((BATCH, SEQ, d), 1.0)
    inp['x_sample'] = nrm((DEC_BATCH, DEC_SEQ, d), 1.0)
    inp['norm_mix'] = gain((DEPTH, d))
    inp['norm_ffn'] = gain((DEPTH, d))
    inp['norm_final'] = gain((d,))
    inp['t5_bias'] = nrm((NUM_BUCKETS, DIFF_HEADS, 2), 0.5)
    diff_w = DIFF_HEADS * 2 * DIFF_DH
    inp['diff_w_qkv'] = nrm((n_a, d, 3 * diff_w), d ** -0.5)
    inp['diff_w_o'] = nrm((n_a, diff_w, d), diff_w ** -0.5)
    inp['diff_lambda'] = nrm((n_a, 4, DIFF_DH), 0.1)
    inp['diff_subln'] = gain((n_a, 2 * DIFF_DH))
    na_w = NA_HEADS * NA_DH
    inp['na_w_qkv'] = nrm((n_b, d, 3 * na_w), d ** -0.5)
    inp['na_w_o'] = nrm((n_b, na_w, d), na_w ** -0.5)
    inp['na_rpb'] = nrm((n_b, NA_HEADS, 2 * NA_KH_MAX - 1, 2 * NA_KW - 1), 0.5)
    inp['mla_w_down'] = nrm((n_c, d, MLA_Q_RANK + MLA_KV_RANK + MLA_ROPE), d ** -0.5)
    inp['mla_q_norm'] = gain((n_c, MLA_Q_RANK))
    inp['mla_kv_norm'] = gain((n_c, MLA_KV_RANK))
    inp['mla_w_uq'] = nrm((n_c, MLA_Q_RANK, MLA_HEADS * (MLA_NOPE + MLA_ROPE)), MLA_Q_RANK ** -0.5)
    inp['mla_w_ukv'] = nrm((n_c, MLA_KV_RANK, MLA_HEADS * (MLA_NOPE + MLA_V)), MLA_KV_RANK ** -0.5)
    inp['mla_w_o'] = nrm((n_c, MLA_HEADS * MLA_V, d), (MLA_HEADS * MLA_V) ** -0.5)
    inp['moe_router'] = nrm((DEPTH, d, N_EXPERTS), d ** -0.5)
    inp['moe_w_gate'] = nrm((DEPTH, N_EXPERTS, d, EXPERT_FF), d ** -0.5)
    inp['moe_w_up'] = nrm((DEPTH, N_EXPERTS, d, EXPERT_FF), d ** -0.5)
    inp['moe_w_down'] = nrm((DEPTH, N_EXPERTS, EXPERT_FF, d), EXPERT_FF ** -0.5)
    return inp


def reference(x_prompt, x_sample, norm_mix, norm_ffn, norm_final, t5_bias,
              diff_w_qkv, diff_w_o, diff_lambda, diff_subln,
              na_w_qkv, na_w_o, na_rpb,
              mla_w_down, mla_q_norm, mla_kv_norm, mla_w_uq, mla_w_ukv, mla_w_o,
              moe_router, moe_w_gate, moe_w_up, moe_w_down):
    params = (norm_mix, norm_ffn, norm_final, t5_bias,
              diff_w_qkv, diff_w_o, diff_lambda, diff_subln,
              na_w_qkv, na_w_o, na_rpb,
              mla_w_down, mla_q_norm, mla_kv_norm, mla_w_uq, mla_w_ukv, mla_w_o,
              moe_router, moe_w_gate, moe_w_up, moe_w_down)
    y_prompt = _trunk(x_prompt, *params)
    y_sample = _trunk(x_sample, *params)
    return (y_prompt, y_sample)
```

```python
import functools
import math

import jax
import jax.numpy as jnp
from jax import lax
from jax.experimental import pallas as pl
from jax.experimental.pallas import tpu as pltpu

F32 = jnp.float32
BF16 = jnp.bfloat16
I32 = jnp.int32

D_MODEL = 1024
DEPTH = 4
N_MIXERS = 3
GRID_W = 64
EPS = 1e-6
DIFF_HEADS = 8
DIFF_DH = 64
DIFF_EPS = 1e-5
NUM_BUCKETS = 32
MAX_DISTANCE = 128
NA_HEADS = 16
NA_DH = 64
NA_KH = 8
NA_KW = 16
MLA_HEADS = 16
MLA_Q_RANK = 384
MLA_KV_RANK = 256
MLA_NOPE = 64
MLA_ROPE = 32
MLA_V = 64
ROPE_THETA = 10000.0
N_EXPERTS = 16
EC_CAPACITY_FACTOR = 2

LANES = 128
VMEM_LIMIT = 52 * 1024 * 1024
NEG_BIG = -1e30

TM = 512
DIFF_T = 256
NA_R = 4
NA_TQ = NA_R * GRID_W
NA_KB = NA_R + NA_KH
NA_TK = NA_KB * GRID_W
MLA_T = 512
SEL_BLK = 128
FFN_TS = 256
FFN_TC = 256
CMB_TB = 256
CMB_CH = 64


def _cparams(sem):
    return pltpu.CompilerParams(dimension_semantics=sem, vmem_limit_bytes=VMEM_LIMIT)


def _dot(a, b):
    return jnp.dot(a, b, preferred_element_type=F32)


def _dot_nt(a, b):
    return lax.dot_general(a, b, (((1,), (1,)), ((), ())), preferred_element_type=F32)


def _rms(x, g, eps):
    ms = jnp.mean(x * x, axis=-1, keepdims=True)
    return x * lax.rsqrt(ms + eps) * g


def _rms_proj_kernel(x_ref, g_ref, w_ref, o_ref, *, tn):
    xn = _rms(x_ref[...], g_ref[...], EPS).astype(BF16)
    for j in range(w_ref.shape[1] // tn):
        o_ref[:, j * tn:(j + 1) * tn] = _dot(xn, w_ref[:, j * tn:(j + 1) * tn]).astype(o_ref.dtype)


def _rms_proj(x, g, w, out_dtype, tn):
    m, k = x.shape
    n = w.shape[1]
    return pl.pallas_call(
        functools.partial(_rms_proj_kernel, tn=tn),
        grid=(m // TM,),
        in_specs=[pl.BlockSpec((TM, k), lambda i: (i, 0)),
                  pl.BlockSpec((1, k), lambda i: (0, 0)),
                  pl.BlockSpec((k, n), lambda i: (0, 0))],
        out_specs=pl.BlockSpec((TM, n), lambda i: (i, 0)),
        out_shape=jax.ShapeDtypeStruct((m, n), out_dtype),
        compiler_params=_cparams(("parallel",)),
        name="rms_proj",
    )(x, g.reshape(1, k), w)


def _out_proj_kernel(a_ref, w_ref, x_ref, o_ref):
    o_ref[...] = x_ref[...] + _dot(a_ref[...], w_ref[...])


def _out_proj_residual(a, w, x):
    m, k = a.shape
    n = w.shape[1]
    return pl.pallas_call(
        _out_proj_kernel,
        grid=(m // TM,),
        in_specs=[pl.BlockSpec((TM, k), lambda i: (i, 0)),
                  pl.BlockSpec((k, n), lambda i: (0, 0)),
                  pl.BlockSpec((TM, n), lambda i: (i, 0))],
        out_specs=pl.BlockSpec((TM, n), lambda i: (i, 0)),
        out_shape=jax.ShapeDtypeStruct((m, n), F32),
        compiler_params=_cparams(("parallel",)),
        name="out_proj_residual",
    )(a, w, x)


def _final_norm_kernel(x_ref, g_ref, o_ref):
    o_ref[...] = _rms(x_ref[...], g_ref[...], EPS)


def _final_norm(x, g):
    m, k = x.shape
    return pl.pallas_call(
        _final_norm_kernel,
        grid=(m // TM,),
        in_specs=[pl.BlockSpec((TM, k), lambda i: (i, 0)), pl.BlockSpec((1, k), lambda i: (0, 0))],
        out_specs=pl.BlockSpec((TM, k), lambda i: (i, 0)),
        out_shape=jax.ShapeDtypeStruct((m, k), F32),
        compiler_params=_cparams(("parallel",)),
        name="final_norm",
    )(x, g.reshape(1, k))


def _t5_bucket(rel):
    half = NUM_BUCKETS // 2
    max_exact = half // 2
    ret = jnp.where(rel > 0, half, 0)
    n = jnp.abs(rel)
    nf = jnp.maximum(n, 1).astype(F32)
    large = max_exact + (jnp.log(nf / max_exact) / math.log(MAX_DISTANCE / max_exact)
                         * (half - max_exact)).astype(I32)
    large = jnp.minimum(large, half - 1)
    return ret + jnp.where(n < max_exact, n, large)


def _t5_tables(t5_bias):
    t = DIFF_T
    rel_tbl = t5_bias[_t5_bucket(jnp.arange(-MAX_DISTANCE, MAX_DISTANCE + 1, dtype=I32))]
    i = jnp.arange(t, dtype=I32)
    tiles = []
    for dl in (-1, 0, 1):
        rel = dl * t + i[None, :] - i[:, None]
        tiles.append(rel_tbl[jnp.clip(rel, -MAX_DISTANCE, MAX_DISTANCE) + MAX_DISTANCE])
    bd = jnp.transpose(jnp.stack(tiles, 0), (3, 4, 0, 1, 2)).astype(F32)
    far = jnp.stack([rel_tbl[0], rel_tbl[2 * MAX_DISTANCE]], axis=-1)
    return bd, far.reshape(-1).astype(F32)


def _diff_attn_kernel(far_ref, q_ref, k_ref, v_ref, bd_ref, lam_ref, sub_ref, o_ref,
                      m_ref, l_ref, acc_ref, *, seq, lambda_init):
    t = DIFF_T
    h = pl.program_id(1)
    qi = pl.program_id(2)
    nk = seq // t
    lane = lax.broadcasted_iota(I32, (1, LANES), 1)
    q = q_ref[...]
    zero = jnp.zeros_like(q)
    qh = (jnp.where(lane < DIFF_DH, q, zero), jnp.where(lane >= DIFF_DH, q, zero))
    scale = DIFF_DH ** -0.5
    m_ref[...] = jnp.full(m_ref.shape, -jnp.inf, F32)
    l_ref[...] = jnp.zeros(l_ref.shape, F32)
    acc_ref[...] = jnp.zeros(acc_ref.shape, F32)

    def tile(kt, bias_of):
        off = pl.multiple_of(kt * t, t)
        k = k_ref[pl.ds(off, t), :]
        v = v_ref[pl.ds(off, t), :]
        for j in range(2):
            s = _dot_nt(qh[j], k) * scale + bias_of(j, kt)
            m_prev = m_ref[j]
            m_new = jnp.maximum(m_prev, jnp.max(s, axis=-1, keepdims=True))
            alpha = jnp.exp(m_prev - m_new)
            p = jnp.exp(s - m_new)
            l_ref[j] = alpha * l_ref[j] + jnp.sum(p, axis=-1, keepdims=True)
            acc_ref[j] = alpha * acc_ref[j] + _dot(p.astype(BF16), v)
            m_ref[j] = m_new

    def far(sign):
        def body(kt, c):
            tile(kt, lambda j, _: far_ref[h * 4 + j * 2 + sign])
            return c
        return body

    def near(kt, c):
        tile(kt, lambda j, kk: bd_ref[0, j, kk - qi + 1])
        return c

    lo = jnp.maximum(qi - 1, 0)
    hi = jnp.minimum(qi + 2, nk)
    lax.fori_loop(0, lo, far(0), 0)
    lax.fori_loop(lo, hi, near, 0)
    lax.fori_loop(hi, nk, far(1), 0)

    lam = lam_ref[...]
    lam_full = (jnp.exp(jnp.sum(lam[0:1] * lam[1:2], axis=-1, keepdims=True))
                - jnp.exp(jnp.sum(lam[2:3] * lam[3:4], axis=-1, keepdims=True)) + lambda_init)
    o = acc_ref[0] / l_ref[0] - lam_full * (acc_ref[1] / l_ref[1])
    o = _rms(o, sub_ref[...], DIFF_EPS) * (1.0 - lambda_init)
    o_ref[...] = o.astype(o_ref.dtype)


def _diff_attention(qkv, bd, far, lam, subln, batch, seq, lambda_init):
    t = DIFF_T
    h = DIFF_HEADS
    nq = seq // t
    grid_spec = pltpu.PrefetchScalarGridSpec(
        num_scalar_prefetch=1,
        grid=(batch, h, nq),
        in_specs=[pl.BlockSpec((t, LANES), lambda b, hh, i, f: (b * nq + i, hh)),
                  pl.BlockSpec((seq, LANES), lambda b, hh, i, f: (b, h + hh)),
                  pl.BlockSpec((seq, LANES), lambda b, hh, i, f: (b, 2 * h + hh)),
                  pl.BlockSpec((1, 2, 3, t, t), lambda b, hh, i, f: (hh, 0, 0, 0, 0)),
                  pl.BlockSpec((4, DIFF_DH), lambda b, hh, i, f: (0, 0)),
                  pl.BlockSpec((1, 2 * DIFF_DH), lambda b, hh, i, f: (0, 0))],
        out_specs=pl.BlockSpec((t, LANES), lambda b, hh, i, f: (b * nq + i, hh)),
        scratch_shapes=[pltpu.VMEM((2, t, 1), F32), pltpu.VMEM((2, t, 1), F32), pltpu.VMEM((2, t, LANES), F32)])
    return pl.pallas_call(
        functools.partial(_diff_attn_kernel, seq=seq, lambda_init=lambda_init),
        grid_spec=grid_spec,
        out_shape=jax.ShapeDtypeStruct((batch * seq, h * LANES), BF16),
        compiler_params=_cparams(("parallel", "parallel", "parallel")),
        name="diff_attention",
    )(far, qkv, qkv, qkv, bd, lam, subln.reshape(1, -1))


def _na_bias_tiles(rpb):
    rows = 4 * NA_KH
    rq = jnp.arange(NA_TQ, dtype=I32) // GRID_W
    cq = jnp.arange(NA_TQ, dtype=I32) % GRID_W
    rk = jnp.arange(NA_TK, dtype=I32) // GRID_W
    ck = jnp.arange(NA_TK, dtype=I32) % GRID_W
    cs = jnp.clip(cq - NA_KW // 2, 0, GRID_W - NA_KW)
    col_ok = (ck[None, :] >= cs[:, None]) & (ck[None, :] < cs[:, None] + NA_KW)
    dc = jnp.clip(ck[None, :] - cq[:, None] + (NA_KW - 1), 0, 2 * NA_KW - 2)
    tiles = []
    for r0, kb0 in ((0, 0), (NA_R, 0), (rows - NA_R, rows - NA_KB)):
        r = r0 + rq
        rs = jnp.clip(r - NA_KH // 2, 0, rows - NA_KH)
        krow = kb0 + rk
        row_ok = (krow[None, :] >= rs[:, None]) & (krow[None, :] < rs[:, None] + NA_KH)
        dr = jnp.clip(krow[None, :] - r[:, None] + (NA_KH - 1), 0, 2 * NA_KH - 2)
        b = rpb[:, dr, dc].astype(F32)
        tiles.append(jnp.where((row_ok & col_ok)[None], b, NEG_BIG))
    return jnp.stack(tiles, 0)


def _na_attn_kernel(q_ref, k_ref, v_ref, b_ref, o_ref, *, nblk):
    i = pl.program_id(2)
    variant = jnp.where(i == 0, 0, jnp.where(i == nblk - 1, 2, 1))
    kb = jnp.clip(i - 1, 0, nblk - NA_KB // NA_R)
    off = pl.multiple_of(kb * NA_TQ, NA_TQ)
    k = k_ref[pl.ds(off, NA_TK), :]
    v = v_ref[pl.ds(off, NA_TK), :]
    q = q_ref[...]
    zero = jnp.zeros_like(q)
    lane = lax.broadcasted_iota(I32, (1, LANES), 1)
    scale = NA_DH ** -0.5
    outs = []
    for hh in range(2):
        in_head = (lane >= hh * NA_DH) & (lane < (hh + 1) * NA_DH)
        s = _dot_nt(jnp.where(in_head, q, zero), k) * scale + b_ref[variant, hh]
        m = jnp.max(s, axis=-1, keepdims=True)
        p = jnp.exp(s - m)
        l = jnp.sum(p, axis=-1, keepdims=True)
        outs.append(_dot(p.astype(BF16), v) / l)
    o_ref[...] = jnp.where(lane < NA_DH, outs[0], outs[1]).astype(o_ref.dtype)


def _na_attention(qkv, bias, batch, seq):
    nblk = seq // NA_TQ
    npair = NA_HEADS // 2
    return pl.pallas_call(
        functools.partial(_na_attn_kernel, nblk=nblk),
        grid=(batch, npair, nblk),
        in_specs=[pl.BlockSpec((NA_TQ, LANES), lambda b, p, i: (b * nblk + i, p)),
                  pl.BlockSpec((seq, LANES), lambda b, p, i: (b, npair + p)),
                  pl.BlockSpec((seq, LANES), lambda b, p, i: (b, 2 * npair + p)),
                  pl.BlockSpec((3, 2, NA_TQ, NA_TK), lambda b, p, i: (0, p, 0, 0))],
        out_specs=pl.BlockSpec((NA_TQ, LANES), lambda b, p, i: (b * nblk + i, p)),
        out_shape=jax.ShapeDtypeStruct((batch * seq, npair * LANES), BF16),
        compiler_params=_cparams(("parallel", "parallel", "parallel")),
        name="na_attention",
    )(qkv, qkv, qkv, bias)


_MLA_DOWN_COLS = MLA_Q_RANK + MLA_KV_RANK + 2 * LANES


def _mla_weights(w_down, w_uq, w_ukv):
    d = w_down.shape[0]
    half = MLA_ROPE // 2
    z = lambda *s: jnp.zeros(s, w_down.dtype)
    kr = w_down[:, MLA_Q_RANK + MLA_KV_RANK:]
    kr_a = jnp.concatenate([z(d, MLA_NOPE), kr, z(d, LANES - MLA_NOPE - MLA_ROPE)], axis=1)
    kr_b = jnp.concatenate([z(d, MLA_NOPE), -kr[:, half:], kr[:, :half], z(d, LANES - MLA_NOPE - MLA_ROPE)], axis=1)
    w_down_ext = jnp.concatenate([w_down[:, :MLA_Q_RANK + MLA_KV_RANK], kr_a, kr_b], axis=1)
    uq = w_uq.reshape(MLA_Q_RANK, MLA_HEADS, MLA_NOPE + MLA_ROPE)
    zq = jnp.zeros((MLA_Q_RANK, MLA_HEADS, LANES - MLA_NOPE - MLA_ROPE), w_uq.dtype)
    uq_a = jnp.concatenate([uq, zq], axis=2).reshape(MLA_Q_RANK, MLA_HEADS * LANES)
    rope = uq[:, :, MLA_NOPE:]
    uq_b = jnp.concatenate([jnp.zeros_like(uq[:, :, :MLA_NOPE]), -rope[:, :, half:], rope[:, :, :half], zq],
                           axis=2).reshape(MLA_Q_RANK, MLA_HEADS * LANES)
    ukv = w_ukv.reshape(MLA_KV_RANK, MLA_HEADS, MLA_NOPE + MLA_V)
    zk = jnp.zeros((MLA_KV_RANK, MLA_HEADS, LANES - MLA_NOPE), w_ukv.dtype)
    uk = jnp.concatenate([ukv[:, :, :MLA_NOPE], zk], axis=2).reshape(MLA_KV_RANK, MLA_HEADS * LANES)
    uv = ukv[:, :, MLA_NOPE:].reshape(MLA_KV_RANK, MLA_HEADS * MLA_V)
    return tuple(w.astype(BF16) for w in (w_down_ext, uq_a, uq_b, uk, uv))


def _rope_tables(seq):
    pos = jnp.arange(seq, dtype=F32)
    inv = ROPE_THETA ** (-jnp.arange(0, MLA_ROPE, 2, dtype=F32) / MLA_ROPE)
    ang = pos[:, None] * inv[None, :]
    cos, sin = jnp.cos(ang), jnp.sin(ang)
    pad = jnp.zeros((seq, LANES - MLA_NOPE - MLA_ROPE), F32)
    ctab = jnp.concatenate([jnp.ones((seq, MLA_NOPE), F32), cos, cos, pad], axis=1)
    stab = jnp.concatenate([jnp.zeros((seq, MLA_NOPE), F32), sin, sin, pad], axis=1)
    return ctab, stab


def _mla_up_kernel(dn_ref, qn_ref, kvn_ref, uqa_ref, uqb_ref, uk_ref, uv_ref, c_ref, s_ref,
                   q_ref, k_ref, v_ref):
    dn = dn_ref[...]
    cq = _rms(dn[:, :MLA_Q_RANK], qn_ref[...], EPS).astype(BF16)
    ckv = _rms(dn[:, MLA_Q_RANK:MLA_Q_RANK + MLA_KV_RANK], kvn_ref[...], EPS).astype(BF16)
    ctab = c_ref[...]
    stab = s_ref[...]
    base = MLA_Q_RANK + MLA_KV_RANK
    k_rope = dn[:, base:base + LANES] * ctab + dn[:, base + LANES:base + 2 * LANES] * stab
    c2 = jnp.concatenate([ctab, ctab], axis=1)
    s2 = jnp.concatenate([stab, stab], axis=1)
    kr2 = jnp.concatenate([k_rope, k_rope], axis=1)
    scale = (MLA_NOPE + MLA_ROPE) ** -0.5
    w = 2 * LANES
    for j in range(MLA_HEADS // 2):
        cols = slice(j * w, (j + 1) * w)
        q = _dot(cq, uqa_ref[:, cols]) * c2 + _dot(cq, uqb_ref[:, cols]) * s2
        q_ref[:, cols] = (q * scale).astype(q_ref.dtype)
        k_ref[:, cols] = (_dot(ckv, uk_ref[:, cols]) + kr2).astype(k_ref.dtype)
    for j in range(MLA_HEADS * MLA_V // w):
        cols = slice(j * w, (j + 1) * w)
        v_ref[:, cols] = _dot(ckv, uv_ref[:, cols]).astype(v_ref.dtype)


def _mla_up(down, q_norm, kv_norm, uq_a, uq_b, uk, uv, ctab, stab, seq):
    m = down.shape[0]
    nseq = seq // TM
    const = lambda i: (0, 0)
    full = lambda a: pl.BlockSpec(a.shape, const)
    hq = MLA_HEADS * LANES
    return pl.pallas_call(
        _mla_up_kernel,
        grid=(m // TM,),
        in_specs=[pl.BlockSpec((TM, _MLA_DOWN_COLS), lambda i: (i, 0)),
                  pl.BlockSpec((1, MLA_Q_RANK), const), pl.BlockSpec((1, MLA_KV_RANK), const),
                  full(uq_a), full(uq_b), full(uk), full(uv),
                  pl.BlockSpec((TM, LANES), lambda i: (i % nseq, 0)),
                  pl.BlockSpec((TM, LANES), lambda i: (i % nseq, 0))],
        out_specs=[pl.BlockSpec((TM, hq), lambda i: (i, 0)),
                   pl.BlockSpec((TM, hq), lambda i: (i, 0)),
                   pl.BlockSpec((TM, MLA_HEADS * MLA_V), lambda i: (i, 0))],
        out_shape=[jax.ShapeDtypeStruct((m, hq), BF16), jax.ShapeDtypeStruct((m, hq), BF16),
                   jax.ShapeDtypeStruct((m, MLA_HEADS * MLA_V), BF16)],
        compiler_params=_cparams(("parallel",)),
        name="mla_up",
    )(down, q_norm.reshape(1, -1), kv_norm.reshape(1, -1), uq_a, uq_b, uk, uv, ctab, stab)


def _mla_attn_kernel(q_ref, k_ref, v_ref, o_ref, m_ref, l_ref, acc_ref, *, seq):
    t = MLA_T
    m_ref[...] = jnp.full(m_ref.shape, -jnp.inf, F32)
    l_ref[...] = jnp.zeros(l_ref.shape, F32)
    acc_ref[...] = jnp.zeros(acc_ref.shape, F32)

    def body(kt, c):
        off = pl.multiple_of(kt * t, t)
        v = v_ref[pl.ds(off, t), :]
        for hh in range(2):
            cols = slice(hh * LANES, (hh + 1) * LANES)
            s = _dot_nt(q_ref[:, cols], k_ref[pl.ds(off, t), cols])
            m_prev = m_ref[hh]
            m_new = jnp.maximum(m_prev, jnp.max(s, axis=-1, keepdims=True))
            alpha = jnp.exp(m_prev - m_new)
            p = jnp.exp(s - m_new)
            l_ref[hh] = alpha * l_ref[hh] + jnp.sum(p, axis=-1, keepdims=True)
            acc_ref[hh] = alpha * acc_ref[hh] + _dot(p.astype(BF16), v)
            m_ref[hh] = m_new
        return c

    lax.fori_loop(0, seq // t, body, 0)
    lane = lax.broadcasted_iota(I32, (1, LANES), 1)
    o_ref[...] = jnp.where(lane < MLA_V, acc_ref[0] / l_ref[0], acc_ref[1] / l_ref[1]).astype(o_ref.dtype)


def _mla_attention(q, k, v, batch, seq):
    t = MLA_T
    nq = seq // t
    npair = MLA_HEADS // 2
    return pl.pallas_call(
        functools.partial(_mla_attn_kernel, seq=seq),
        grid=(batch, npair, nq),
        in_specs=[pl.BlockSpec((t, 2 * LANES), lambda b, p, i: (b * nq + i, p)),
                  pl.BlockSpec((seq, 2 * LANES), lambda b, p, i: (b, p)),
                  pl.BlockSpec((seq, LANES), lambda b, p, i: (b, p))],
        out_specs=pl.BlockSpec((t, LANES), lambda b, p, i: (b * nq + i, p)),
        out_shape=jax.ShapeDtypeStruct((batch * seq, npair * LANES), BF16),
        scratch_shapes=[pltpu.VMEM((2, t, 1), F32), pltpu.VMEM((2, t, 1), F32), pltpu.VMEM((2, t, LANES), F32)],
        compiler_params=_cparams(("parallel", "parallel", "parallel")),
        name="mla_attention",
    )(q, k, v)


def _router_kernel(x_ref, g_ref, r_ref, xn_ref, aff_ref):
    xn = _rms(x_ref[...], g_ref[...], EPS)
    xn_ref[...] = xn.astype(xn_ref.dtype)
    logits = jnp.dot(xn, r_ref[...], preferred_element_type=F32, precision=lax.Precision.HIGHEST)
    mx = jnp.max(logits, axis=-1, keepdims=True)
    ex = jnp.exp(logits - mx)
    aff_ref[...] = ex / jnp.sum(ex, axis=-1, keepdims=True)


def _router(x, g, router):
    n, d = x.shape
    e = router.shape[1]
    return pl.pallas_call(
        _router_kernel,
        grid=(n // TM,),
        in_specs=[pl.BlockSpec((TM, d), lambda i: (i, 0)), pl.BlockSpec((1, d), lambda i: (0, 0)),
                  pl.BlockSpec((d, e), lambda i: (0, 0))],
        out_specs=[pl.BlockSpec((TM, d), lambda i: (i, 0)), pl.BlockSpec((TM, e), lambda i: (i, 0))],
        out_shape=[jax.ShapeDtypeStruct((n, d), BF16), jax.ShapeDtypeStruct((n, e), F32)],
        compiler_params=_cparams(("parallel",)),
        name="moe_router",
    )(x, g.reshape(1, d), router)


def _select_kernel(aff_ref, pos_ref, cum_ref, cnt_ref, *, cap, nb):
    ne = aff_ref.shape[0]
    bits = pltpu.bitcast(aff_ref[...], I32)

    def count(mask):
        part = jnp.sum(jnp.where(mask, 1.0, 0.0), axis=1, keepdims=True)
        return jnp.sum(part, axis=2, keepdims=True)

    def search(i, cur):
        cand = cur | jnp.left_shift(jnp.int32(1), 30 - i)
        return jnp.where(count(bits >= cand) >= cap, cand, cur)

    thr = lax.fori_loop(0, 31, search, jnp.zeros((ne, 1, 1), I32))
    gt = bits > thr
    eq = bits == thr
    need = cap - count(gt)

    a = lax.broadcasted_iota(I32, (SEL_BLK, SEL_BLK), 0)
    b = lax.broadcasted_iota(I32, (SEL_BLK, SEL_BLK), 1)
    before = jnp.where(a < b, 1.0, 0.0).astype(BF16)
    ones = jnp.ones((SEL_BLK, SEL_BLK), BF16)
    a2 = lax.broadcasted_iota(I32, (nb, nb), 0)
    b2 = lax.broadcasted_iota(I32, (nb, nb), 1)
    blocks_before = jnp.where(b2 < a2, 1.0, 0.0).astype(BF16)

    def ranks(mask):
        mb = jnp.where(mask, 1.0, 0.0).astype(BF16).reshape(ne * nb, SEL_BLK)
        local = _dot(mb, before).reshape(ne, nb, SEL_BLK)
        cnt = _dot(mb, ones).reshape(ne, nb, SEL_BLK)
        for e in range(ne):
            cum_ref[e] = _dot(blocks_before, cnt[e].astype(BF16))
        return cum_ref[...] + local, cnt

    eq_rank, _ = ranks(eq)
    sel = gt | (eq & (eq_rank < need))
    rank, cnt = ranks(sel)
    cnt_ref[...] = cnt
    pos_ref[...] = jnp.where(sel, rank, -1.0).astype(I32)


def _select(aff_t3, cap):
    ne, nb, _ = aff_t3.shape
    shp = (ne, nb, SEL_BLK)
    return pl.pallas_call(
        functools.partial(_select_kernel, cap=cap, nb=nb),
        out_shape=[jax.ShapeDtypeStruct(shp, I32), jax.ShapeDtypeStruct(shp, F32), jax.ShapeDtypeStruct(shp, F32)],
        compiler_params=pltpu.CompilerParams(vmem_limit_bytes=VMEM_LIMIT),
        name="moe_select",
    )(aff_t3)


def _ffn_kernel(lo_ref, hi_ref, xn_hbm, pos_hbm, wg_ref, wu_ref, wd_ref, o_ref, xbuf, pbuf, xg_ref, sem, *, ntile):
    e = pl.program_id(0)
    j = pl.program_id(1)
    c0 = lo_ref[e * ntile + j]
    c1 = hi_ref[e * ntile + j]

    def copies(c, slot):
        off = pl.multiple_of(c * FFN_TC, FFN_TC)
        return (pltpu.make_async_copy(xn_hbm.at[pl.ds(off, FFN_TC), :], xbuf.at[slot], sem.at[0, slot]),
                pltpu.make_async_copy(pos_hbm.at[pl.ds(e, 1), pl.ds(off, FFN_TC)], pbuf.at[slot], sem.at[1, slot]))

    def start(c, slot):
        for cp in copies(c, slot):
            cp.start()

    start(c0, 0)
    xg_ref[...] = jnp.zeros(xg_ref.shape, F32)
    slot_id = j * FFN_TS + lax.broadcasted_iota(I32, (FFN_TS, 1), 0)

    def body(c, carry):
        slot = (c - c0) % 2
        for cp in copies(c, slot):
            cp.wait()

        @pl.when(c + 1 < c1)
        def _():
            start(c + 1, 1 - slot)

        onehot = jnp.where(pbuf[slot] == slot_id, 1.0, 0.0).astype(BF16)
        xg_ref[...] += _dot(onehot, xbuf[slot])
        return carry

    lax.fori_loop(c0, c1, body, 0)
    xg = xg_ref[...].astype(BF16)
    g = _dot(xg, wg_ref[0])
    u = _dot(xg, wu_ref[0])
    hidden = (g * jax.nn.sigmoid(g) * u).astype(BF16)
    o_ref[...] = _dot(hidden, wd_ref[0]).astype(o_ref.dtype)


def _ffn(xn, pos, lo, hi, wg, wu, wd, cap):
    ne, d, f = wg.shape
    ntile = cap // FFN_TS
    grid_spec = pltpu.PrefetchScalarGridSpec(
        num_scalar_prefetch=2,
        grid=(ne, ntile),
        in_specs=[pl.BlockSpec(memory_space=pl.ANY), pl.BlockSpec(memory_space=pl.ANY),
                  pl.BlockSpec((1, d, f), lambda e, j, lo, hi: (e, 0, 0)),
                  pl.BlockSpec((1, d, f), lambda e, j, lo, hi: (e, 0, 0)),
                  pl.BlockSpec((1, f, d), lambda e, j, lo, hi: (e, 0, 0))],
        out_specs=pl.BlockSpec((FFN_TS, d), lambda e, j, lo, hi: (e * ntile + j, 0)),
        scratch_shapes=[pltpu.VMEM((2, FFN_TC, d), BF16), pltpu.VMEM((2, 1, FFN_TC), I32),
                        pltpu.VMEM((FFN_TS, d), F32), pltpu.SemaphoreType.DMA((2, 2))])
    return pl.pallas_call(
        functools.partial(_ffn_kernel, ntile=ntile),
        grid_spec=grid_spec,
        out_shape=jax.ShapeDtypeStruct((ne * cap, d), BF16),
        compiler_params=_cparams(("arbitrary", "arbitrary")),
        name="moe_ffn",
    )(lo, hi, xn, pos, wg, wu, wd)


def _combine_kernel(base_ref, cnt_ref, x_ref, post_ref, aff_ref, ye_hbm, o_ref, stage, extra, acc_ref, sem, xsem,
                    *, ntb, cap):
    tb = pl.program_id(0)
    ne = post_ref.shape[1]
    last_row = ne * cap - CMB_CH
    lane = lax.broadcasted_iota(I32, (1, CMB_CH), 1)

    def window(e, blk, ch):
        first = (base_ref[e * ntb + blk] // 16) * 16 + ch * CMB_CH
        row = pl.multiple_of(jnp.minimum(e * cap + first, last_row), 16)
        slot = row - e * cap + lane
        return row, jnp.where(slot >= first, slot, -2)

    def fetch(blk, buf):
        for e in range(ne):
            row, _ = window(e, blk, 0)
            pltpu.make_async_copy(ye_hbm.at[pl.ds(row, CMB_CH), :], stage.at[buf, e], sem.at[buf, e]).start()

    @pl.when(tb == 0)
    def _():
        fetch(0, 0)

    @pl.when(tb + 1 < ntb)
    def _():
        fetch(tb + 1, (tb + 1) % 2)

    buf = tb % 2
    acc_ref[...] = x_ref[...]
    for e in range(ne):
        pcol = post_ref[:, e:e + 1]
        gate = aff_ref[:, e:e + 1]
        pltpu.make_async_copy(ye_hbm.at[pl.ds(0, CMB_CH), :], stage.at[buf, e], sem.at[buf, e]).wait()
        _, slots = window(e, tb, 0)
        onehot = jnp.where(pcol == slots, 1.0, 0.0).astype(BF16)
        acc_ref[...] += gate * _dot(onehot, stage[buf, e])

        first = (base_ref[e * ntb + tb] // 16) * 16
        nch = (base_ref[e * ntb + tb] + cnt_ref[e * ntb + tb] - first + CMB_CH - 1) // CMB_CH

        def more(ch, carry, e=e, pcol=pcol, gate=gate):
            row, slots = window(e, tb, ch)
            cp = pltpu.make_async_copy(ye_hbm.at[pl.ds(row, CMB_CH), :], extra, xsem.at[0])
            cp.start()
            cp.wait()
            onehot = jnp.where(pcol == slots, 1.0, 0.0).astype(BF16)
            acc_ref[...] += gate * _dot(onehot, extra[...])
            return carry

        lax.fori_loop(1, nch, more, 0)
    o_ref[...] = acc_ref[...]


def _combine(x, pos_t, aff, ye, base, cnt, cap):
    n, d = x.shape
    ne = aff.shape[1]
    ntb = n // CMB_TB
    grid_spec = pltpu.PrefetchScalarGridSpec(
        num_scalar_prefetch=2,
        grid=(ntb,),
        in_specs=[pl.BlockSpec((CMB_TB, d), lambda i, b, c: (i, 0)),
                  pl.BlockSpec((CMB_TB, ne), lambda i, b, c: (i, 0)),
                  pl.BlockSpec((CMB_TB, ne), lambda i, b, c: (i, 0)),
                  pl.BlockSpec(memory_space=pl.ANY)],
        out_specs=pl.BlockSpec((CMB_TB, d), lambda i, b, c: (i, 0)),
        scratch_shapes=[pltpu.VMEM((2, ne, CMB_CH, d), BF16), pltpu.VMEM((CMB_CH, d), BF16),
                        pltpu.VMEM((CMB_TB, d), F32), pltpu.SemaphoreType.DMA((2, ne)),
                        pltpu.SemaphoreType.DMA((1,))])
    return pl.pallas_call(
        functools.partial(_combine_kernel, ntb=ntb, cap=cap),
        grid_spec=grid_spec,
        out_shape=jax.ShapeDtypeStruct((n, d), F32),
        compiler_params=_cparams(("arbitrary",)),
        name="moe_combine",
    )(base, cnt, x, pos_t, aff, ye)


def _expert_choice(x, g, router, wg, wu, wd):
    n = x.shape[0]
    ne = router.shape[1]
    cap = EC_CAPACITY_FACTOR * n // ne
    nb = n // SEL_BLK
    xn, aff = _router(x, g, router)
    pos3, cum3, cnt3 = _select(jnp.transpose(aff).reshape(ne, nb, SEL_BLK), cap)
    pos = pos3.reshape(ne, n)

    cume = cum3[:, :, 0].astype(I32)
    cumi = cume + cnt3[:, :, 0].astype(I32)
    tile_start = jnp.arange(cap // FFN_TS, dtype=I32) * FFN_TS
    blk_lo = jnp.sum(cumi[:, None, :] <= tile_start[None, :, None], axis=2)
    blk_hi = jnp.sum(cume[:, None, :] < (tile_start + FFN_TS)[None, :, None], axis=2)
    per = FFN_TC // SEL_BLK
    chunk_lo = (blk_lo // per).astype(I32).reshape(-1)
    chunk_hi = ((blk_hi + per - 1) // per).astype(I32).reshape(-1)
    ye = _ffn(xn, pos, chunk_lo, chunk_hi, wg, wu, wd, cap)

    per = CMB_TB // SEL_BLK
    base = cume[:, ::per].reshape(-1)
    cnt = (cumi[:, per - 1::per] - cume[:, ::per]).reshape(-1)
    return _combine(x, jnp.transpose(pos), aff, ye, base, cnt, cap)


def _trunk(x3, p):
    batch, seq, d = x3.shape
    x = x3.reshape(batch * seq, d)
    for i in range(DEPTH):
        m, j = i % N_MIXERS, i // N_MIXERS
        if m == 0:
            lambda_init = 0.8 - 0.6 * math.exp(-0.3 * i)
            qkv = _rms_proj(x, p["norm_mix"][i], p["diff_w_qkv"][j], BF16, 512)
            a = _diff_attention(qkv, p["t5_bd"], p["t5_far"], p["diff_lambda"][j], p["diff_subln"][j],
                                batch, seq, lambda_init)
            x = _out_proj_residual(a, p["diff_w_o"][j], x)
        elif m == 1:
            qkv = _rms_proj(x, p["norm_mix"][i], p["na_w_qkv"][j], BF16, 512)
            a = _na_attention(qkv, p["na_bias"][j], batch, seq)
            x = _out_proj_residual(a, p["na_w_o"][j], x)
        else:
            w_down_ext, uq_a, uq_b, uk, uv = p["mla_w"][j]
            down = _rms_proj(x, p["norm_mix"][i], w_down_ext, F32, _MLA_DOWN_COLS)
            ctab, stab = p["rope"]
            q, k, v = _mla_up(down, p["mla_q_norm"][j], p["mla_kv_norm"][j], uq_a, uq_b, uk, uv,
                              ctab[:seq], stab[:seq], seq)
            a = _mla_attention(q, k, v, batch, seq)
            x = _out_proj_residual(a, p["mla_w_o"][j], x)
        x = _expert_choice(x, p["norm_ffn"][i], p["moe_router"][i], p["moe_w_gate"][i], p["moe_w_up"][i],
                           p["moe_w_down"][i])
    return _final_norm(x, p["norm_final"]).reshape(batch, seq, d)


def kernel(x_prompt, x_sample, norm_mix, norm_ffn, norm_final, t5_bias, diff_w_qkv, diff_w_o, diff_lambda, diff_subln, na_w_qkv, na_w_o, na_rpb, mla_w_down, mla_q_norm, mla_kv_norm, mla_w_uq, mla_w_ukv, mla_w_o, moe_router, moe_w_gate, moe_w_up, moe_w_down):
    t5_bd, t5_far = _t5_tables(t5_bias)
    p = dict(
        norm_mix=norm_mix, norm_ffn=norm_ffn, norm_final=norm_final,
        t5_bd=t5_bd, t5_far=t5_far,
        diff_w_qkv=diff_w_qkv.astype(BF16), diff_w_o=diff_w_o.astype(BF16),
        diff_lambda=diff_lambda, diff_subln=diff_subln,
        na_w_qkv=na_w_qkv.astype(BF16), na_w_o=na_w_o.astype(BF16),
        na_bias=[_na_bias_tiles(na_rpb[j]) for j in range(na_rpb.shape[0])],
        mla_w=[_mla_weights(mla_w_down[j], mla_w_uq[j], mla_w_ukv[j]) for j in range(mla_w_down.shape[0])],
        mla_q_norm=mla_q_norm, mla_kv_norm=mla_kv_norm, mla_w_o=mla_w_o.astype(BF16),
        rope=_rope_tables(max(x_prompt.shape[1], x_sample.shape[1])),
        moe_router=moe_router, moe_w_gate=moe_w_gate.astype(BF16), moe_w_up=moe_w_up.astype(BF16),
        moe_w_down=moe_w_down.astype(BF16),
    )
    return (_trunk(x_prompt, p), _trunk(x_sample, p))
```
